```python
import math
import jax, jax.numpy as jnp
from jax import lax
import numpy as np

D_MODEL = 1024
BATCH = 16
SEQ = 4096
DEPTH = 4
DEC_BATCH = 4
DEC_SEQ = 8192
PAST_LEN = 128

N_HEADS = 8
N_KV_HEADS = 2
HEAD_DIM = 64
ATT_WIDTH = N_HEADS * HEAD_DIM
KV_WIDTH = N_KV_HEADS * HEAD_DIM
WINDOW = 128
BLOCK = 128
ROPE_DIM = HEAD_DIM // 4
ROPE_THETA = 500000.0
F_GROUPS = 4
F_GROUP_DIM = 128
F_WIDTH = F_GROUPS * F_GROUP_DIM
PLE_DIM = 256
EPS = 1e-6

SPLITS = [ATT_WIDTH, KV_WIDTH, KV_WIDTH, ATT_WIDTH, F_WIDTH, F_WIDTH, D_MODEL, D_MODEL]
IN_WIDTH = sum(SPLITS)
SPLIT_IDX = list(np.cumsum(SPLITS)[:-1])

kernel_name = "hybrid_window_gqa_fnet_gated_encoder"


def rmsnorm(x, g):
    xf = x.astype(jnp.float32)
    y = xf * lax.rsqrt(jnp.mean(xf * xf, axis=-1, keepdims=True) + EPS)
    return (y * g.astype(jnp.float32)).astype(x.dtype)


def partial_rope(x, pos):
    half = ROPE_DIM // 2
    inv_freq = ROPE_THETA ** (-jnp.arange(0, ROPE_DIM, 2, dtype=jnp.float32) / ROPE_DIM)
    ang = pos.astype(jnp.float32)[:, None] * inv_freq[None, :]
    cos = jnp.cos(ang)[None, :, None, :].astype(x.dtype)
    sin = jnp.sin(ang)[None, :, None, :].astype(x.dtype)
    x1 = x[..., :half]
    x2 = x[..., half:ROPE_DIM]
    return jnp.concatenate([x1 * cos - x2 * sin, x2 * cos + x1 * sin, x[..., ROPE_DIM:]], axis=-1)


def band_attention(q, k, v, sink):
    B, S = q.shape[0], q.shape[1]
    nb = S // BLOCK
    R = N_HEADS // N_KV_HEADS
    qb = q.reshape(B, nb, BLOCK, N_KV_HEADS, R, HEAD_DIM)
    pad = ((0, 0), (BLOCK, BLOCK), (0, 0), (0, 0))
    kp = jnp.pad(k, pad).reshape(B, nb + 2, BLOCK, N_KV_HEADS, HEAD_DIM)
    vp = jnp.pad(v, pad).reshape(B, nb + 2, BLOCK, N_KV_HEADS, HEAD_DIM)
    kw = jnp.concatenate([kp[:, :-2], kp[:, 1:-1], kp[:, 2:]], axis=2)
    vw = jnp.concatenate([vp[:, :-2], vp[:, 1:-1], vp[:, 2:]], axis=2)
    s = jnp.einsum('bnqgrd,bnkgd->bngrqk', qb, kw).astype(jnp.float32) * (1.0 / math.sqrt(HEAD_DIM))
    qi = jnp.arange(BLOCK)[:, None]
    ki = jnp.arange(3 * BLOCK)[None, :]
    rel = ki - BLOCK - qi
    jpos = jnp.arange(nb)[:, None, None] * BLOCK + (ki - BLOCK)[None]
    valid = (jnp.abs(rel)[None] <= WINDOW) & (jpos >= 0) & (jpos < S)
    s = jnp.where(valid[None, :, None, None], s, -1e30)
    sk = sink.astype(jnp.float32).reshape(N_KV_HEADS, R)[None, None, :, :, None, None]
    m = jnp.maximum(jnp.max(s, axis=-1, keepdims=True), sk)
    e = jnp.exp(s - m)
    pr = (e / (jnp.sum(e, axis=-1, keepdims=True) + jnp.exp(sk - m))).astype(v.dtype)
    o = jnp.einsum('bngrqk,bnkgd->bnqgrd', pr, vw)
    return o.reshape(B, S, ATT_WIDTH)


def fourier_mix(u, w_fmix):
    B, S = u.shape[0], u.shape[1]
    ug = u.reshape(B, S, F_GROUPS, F_GROUP_DIM).astype(jnp.float32)
    f = jnp.real(jnp.fft.fft2(ug, axes=(1, 3), norm='ortho')).astype(u.dtype)
    y = jnp.einsum('bsgc,gcd->bsgd', f, w_fmix)
    return y.reshape(B, S, F_WIDTH)


def trunk(x, p, ln1, w_in, sink, w_fmix, w_ao, w_fo, w_out, w_pe, ln_pg, w_pg, ln_f):
    B, S, _ = x.shape
    pos = jnp.arange(S)
    h = x
    for i in range(DEPTH):
        hn = rmsnorm(h, ln1[i])
        z = hn @ w_in[i]
        q, k, v, ga, uf, gf, mga, mgf = jnp.split(z, SPLIT_IDX, axis=-1)
        q = partial_rope(q.reshape(B, S, N_HEADS, HEAD_DIM), pos)
        k = partial_rope(k.reshape(B, S, N_KV_HEADS, HEAD_DIM), pos)
        v = v.reshape(B, S, N_KV_HEADS, HEAD_DIM)
        a = band_attention(q, k, v, sink[i]) * jax.nn.silu(ga)
        f = fourier_mix(uf, w_fmix[i]) * jax.nn.silu(gf)
        merged = jax.nn.sigmoid(mga) * (a @ w_ao[i]) + jax.nn.sigmoid(mgf) * (f @ w_fo[i])
        h = h + merged @ w_out[i]
        gate = jax.nn.sigmoid(rmsnorm(h, ln_pg[i]) @ w_pg[i])
        h = h + (p[i].astype(h.dtype) @ w_pe[i]) * gate
    return rmsnorm(h, ln_f)


def setup_inputs(seed: int = 0) -> dict:
    key = jax.random.key(seed)
    ks = jax.random.split(key, 16)
    f32 = jnp.float32
    nrm = lambda k, shape, scale: jax.random.normal(k, shape, f32) * scale
    return {
        "x_prompt": nrm(ks[0], (BATCH, SEQ, D_MODEL), 1.0),
        "x_sample": nrm(ks[1], (DEC_BATCH, DEC_SEQ, D_MODEL), 1.0),
        "p_prompt": nrm(ks[2], (DEPTH, BATCH, SEQ, PLE_DIM), 1.0),
        "p_sample": nrm(ks[3], (DEPTH, DEC_BATCH, DEC_SEQ, PLE_DIM), 1.0),
        "ln1": 1.0 + nrm(ks[4], (DEPTH, D_MODEL), 0.01),
        "w_in": nrm(ks[5], (DEPTH, D_MODEL, IN_WIDTH), D_MODEL ** -0.5),
        "sink": nrm(ks[6], (DEPTH, N_HEADS), 0.5),
        "w_fmix": nrm(ks[7], (DEPTH, F_GROUPS, F_GROUP_DIM, F_GROUP_DIM), F_GROUP_DIM ** -0.5),
        "w_ao": nrm(ks[8], (DEPTH, ATT_WIDTH, D_MODEL), ATT_WIDTH ** -0.5),
        "w_fo": nrm(ks[9], (DEPTH, F_WIDTH, D_MODEL), F_WIDTH ** -0.5),
        "w_out": nrm(ks[10], (DEPTH, D_MODEL, D_MODEL), 0.5 * D_MODEL ** -0.5),
        "w_pe": nrm(ks[11], (DEPTH, PLE_DIM, D_MODEL), 0.5 * PLE_DIM ** -0.5),
        "ln_pg": 1.0 + nrm(ks[12], (DEPTH, D_MODEL), 0.01),
        "w_pg": nrm(ks[13], (DEPTH, D_MODEL, D_MODEL), D_MODEL ** -0.5),
        "ln_f": 1.0 + nrm(ks[14], (D_MODEL,), 0.01),
    }


def reference(x_prompt, x_sample, p_prompt, p_sample, ln1, w_in, sink, w_fmix, w_ao, w_fo,
              w_out, w_pe, ln_pg, w_pg, ln_f):
    y_prompt = trunk(x_prompt, p_prompt, ln1, w_in, sink, w_fmix, w_ao, w_fo, w_out, w_pe, ln_pg, w_pg, ln_f)
    y_sample = trunk(x_sample, p_sample, ln1, w_in, sink, w_fmix, w_ao, w_fo, w_out, w_pe, ln_pg, w_pg, ln_f)
    return (y_prompt, y_sample)
```

```python
import functools
import math

import numpy as np
import jax
import jax.numpy as jnp
from jax import lax
from jax.experimental import pallas as pl
from jax.experimental.pallas import tpu as pltpu

D_MODEL = 1024
DEPTH = 4
N_HEADS = 8
N_KV_HEADS = 2
HEAD_DIM = 64
ATT_WIDTH = N_HEADS * HEAD_DIM
KV_WIDTH = N_KV_HEADS * HEAD_DIM
WINDOW = 128
ROPE_DIM = 16
ROPE_THETA = 500000.0
F_GROUPS = 4
F_GROUP_DIM = 128
F_WIDTH = F_GROUPS * F_GROUP_DIM
PLE_DIM = 256
EPS = 1e-6
MASK_VALUE = -1e30

F32 = jnp.float32
BF16 = jnp.bfloat16

LANES = 128
ROW_TILE = 512
FFT_FAST = 64
FFT1_LANES = 8192
FFT2_ROWS = 8
VMEM_LIMIT = 56 * 1024 * 1024

_Q0, _GA0, _GF0, _MGA0, _MGF0, _MAIN_W = 0, 512, 1024, 1536, 2560, 3584
NEXT_W = 2 * KV_WIDTH + F_WIDTH


def _dot(a, b):
    return jnp.dot(a, b, preferred_element_type=F32)


def _sigmoid(x):
    return 0.5 * (jnp.tanh(0.5 * x) + 1.0)


def _silu(x):
    return x * _sigmoid(x)


def _rms(x, g):
    ms = jnp.mean(x * x, axis=-1, keepdims=True)
    return x * lax.rsqrt(ms + EPS) * g


def _rope(x, r_ref):
    c, sa, sb = r_ref[0], r_ref[1], r_ref[2]
    cols = []
    for j in range(x.shape[1] // LANES):
        xj = x[:, j * LANES:(j + 1) * LANES]
        cols.append(xj * c
                    + pltpu.roll(xj, ROPE_DIM // 2, 1) * sa
                    + pltpu.roll(xj, LANES - ROPE_DIM // 2, 1) * sb)
    return cols[0] if len(cols) == 1 else jnp.concatenate(cols, axis=1)


@functools.lru_cache(maxsize=None)
def _rope_tables(S, scale):
    half = ROPE_DIM // 2
    inv = ROPE_THETA ** (-np.arange(0, ROPE_DIM, 2, dtype=np.float64) / ROPE_DIM)
    ang = np.arange(S, dtype=np.float64)[:, None] * inv[None, :]
    cos, sin = np.cos(ang), np.sin(ang)
    c = np.ones((S, LANES))
    sa = np.zeros((S, LANES))
    sb = np.zeros((S, LANES))
    for lane in range(LANES):
        m = lane % HEAD_DIM
        if m < half:
            c[:, lane] = cos[:, m]
            sb[:, lane] = -sin[:, m]
        elif m < ROPE_DIM:
            c[:, lane] = cos[:, m - half]
            sa[:, lane] = sin[:, m - half]
    return (np.stack([c, sa, sb]) * scale).astype(np.float32)


@functools.lru_cache(maxsize=None)
def _fft_tables(S):
    B = FFT_FAST
    A = S // B
    k1 = np.arange(A)
    ang1 = 2.0 * np.pi * ((k1[:, None] * k1[None, :]) % A) / A
    f1 = np.concatenate([np.cos(ang1), -np.sin(ang1)], axis=0) / math.sqrt(A)
    k2 = np.arange(B)
    b = np.arange(B)
    kk = k1[:, None, None] + A * k2[None, :, None]
    ang2 = 2.0 * np.pi * ((kk * b[None, None, :]) % S) / S
    gre, gim = np.cos(ang2), -np.sin(ang2)
    l1 = np.concatenate([gre, gim], axis=1) / math.sqrt(B)
    l2 = np.concatenate([-gim, gre], axis=1) / math.sqrt(B)
    c = np.arange(F_GROUP_DIM)
    ang3 = 2.0 * np.pi * ((c[:, None] * c[None, :]) % F_GROUP_DIM) / F_GROUP_DIM
    cc = np.cos(ang3) / math.sqrt(F_GROUP_DIM)
    sc = np.sin(ang3) / math.sqrt(F_GROUP_DIM)
    return tuple(np.asarray(t, dtype=BF16) for t in (f1, l1, l2, cc, sc))


def _const_spec(shape):
    zeros = (0,) * len(shape)
    return pl.BlockSpec(shape, lambda *_: zeros, pipeline_mode=pl.Buffered(1))


def _fft1_kernel(x_ref, f1_ref, yre_ref, yim_ref):
    a = x_ref.shape[0]
    y = _dot(f1_ref[...], x_ref[...])
    yre_ref[...] = y[:a].astype(BF16)
    yim_ref[...] = y[a:].astype(BF16)


def _fft_stage1(u, f1):
    nb, S, _ = u.shape
    B = FFT_FAST
    A = S // B
    width = B * F_WIDTH
    L = min(FFT1_LANES, width)
    x = u.reshape(nb, A, width)
    spec = pl.BlockSpec((None, A, L), lambda i, j: (i, 0, j))
    yre, yim = pl.pallas_call(
        _fft1_kernel,
        grid=(nb, width // L),
        in_specs=[spec, _const_spec((2 * A, A))],
        out_specs=[spec, spec],
        out_shape=[jax.ShapeDtypeStruct((nb, A, width), BF16)] * 2,
        compiler_params=pltpu.CompilerParams(
            dimension_semantics=("parallel", "parallel"), vmem_limit_bytes=VMEM_LIMIT),
        name="fft_stage1",
    )(x, f1)
    return yre.reshape(nb, S, F_WIDTH), yim.reshape(nb, S, F_WIDTH)


def _fft2_kernel(yre_ref, yim_ref, l1_ref, l2_ref, cc_ref, sc_ref, wm_ref, o_ref):
    nk = l1_ref.shape[0]
    B = l1_ref.shape[2]
    zre, zim = [], []
    for k in range(nk):
        rows = slice(k * B, (k + 1) * B)
        z = _dot(l1_ref[k], yre_ref[rows, :]) + _dot(l2_ref[k], yim_ref[rows, :])
        zre.append(z[:B])
        zim.append(z[B:])
    zre = jnp.concatenate(zre, axis=0).astype(BF16)
    zim = jnp.concatenate(zim, axis=0).astype(BF16)
    for g in range(F_GROUPS):
        lanes = slice(g * F_GROUP_DIM, (g + 1) * F_GROUP_DIM)
        r = _dot(zre[:, lanes], cc_ref[...]) + _dot(zim[:, lanes], sc_ref[...])
        y = _dot(r.astype(BF16), wm_ref[g]).astype(BF16)
        for k in range(nk):
            o_ref[:, k * F_WIDTH + g * F_GROUP_DIM:k * F_WIDTH + (g + 1) * F_GROUP_DIM] = (
                y[k * B:(k + 1) * B])


def _fft_stage2(yre, yim, l1, l2, cc, sc, wm):
    nb, S, _ = yre.shape
    B = FFT_FAST
    A = S // B
    nk = min(FFT2_ROWS, A)
    yspec = pl.BlockSpec((None, nk * B, F_WIDTH), lambda i, j: (i, j, 0))
    lspec = pl.BlockSpec((nk, 2 * B, B), lambda i, j: (j, 0, 0))
    out = pl.pallas_call(
        _fft2_kernel,
        grid=(nb, A // nk),
        in_specs=[yspec, yspec, lspec, lspec,
                  _const_spec((F_GROUP_DIM, F_GROUP_DIM)), _const_spec((F_GROUP_DIM, F_GROUP_DIM)),
                  _const_spec((F_GROUPS, F_GROUP_DIM, F_GROUP_DIM))],
        out_specs=pl.BlockSpec((None, B, nk * F_WIDTH), lambda i, j: (i, 0, j)),
        out_shape=jax.ShapeDtypeStruct((nb, B, A * F_WIDTH), BF16),
        compiler_params=pltpu.CompilerParams(
            dimension_semantics=("parallel", "parallel"), vmem_limit_bytes=VMEM_LIMIT),
        name="fft_stage2",
    )(yre, yim, l1, l2, cc, sc, wm)
    return out.reshape(nb, S, F_WIDTH)


def _fourier_mix(u, wm, tables):
    f1, l1, l2, cc, sc = tables
    yre, yim = _fft_stage1(u, f1)
    return _fft_stage2(yre, yim, l1, l2, cc, sc, wm)


def _project_next(h, ln_ref, wnext_ref, rk_ref, kv_ref, u_ref):
    hn = _rms(h, ln_ref[...]).astype(BF16)
    z = _dot(hn, wnext_ref[...])
    k = _rope(z[:, :KV_WIDTH], rk_ref)
    kv_ref[:, :KV_WIDTH] = k.astype(BF16)
    kv_ref[:, KV_WIDTH:] = z[:, KV_WIDTH:2 * KV_WIDTH].astype(BF16)
    u_ref[...] = z[:, 2 * KV_WIDTH:].astype(BF16)


def _prologue_kernel(x_ref, ln_ref, wnext_ref, rk_ref, kv_ref, u_ref):
    _project_next(x_ref[...], ln_ref, wnext_ref, rk_ref, kv_ref, u_ref)


def _prologue(x, ln, wnext, ropek):
    nb, S, _ = x.shape
    tm = min(ROW_TILE, S)
    row = lambda w: pl.BlockSpec((None, tm, w), lambda b, t: (b, t, 0))
    return pl.pallas_call(
        _prologue_kernel,
        grid=(nb, S // tm),
        in_specs=[row(D_MODEL), _const_spec((1, D_MODEL)), _const_spec((D_MODEL, NEXT_W)),
                  pl.BlockSpec((3, tm, LANES), lambda b, t: (0, t, 0))],
        out_specs=[row(2 * KV_WIDTH), row(F_WIDTH)],
        out_shape=[jax.ShapeDtypeStruct((nb, S, 2 * KV_WIDTH), BF16),
                   jax.ShapeDtypeStruct((nb, S, F_WIDTH), BF16)],
        compiler_params=pltpu.CompilerParams(
            dimension_semantics=("parallel", "parallel"), vmem_limit_bytes=VMEM_LIMIT),
        name="prologue",
    )(x, ln, wnext, ropek)


def _attention_block(j, seq_len, q, kv_buf, sink_ref, attn_buf):
    tm = q.shape[0]
    t = pl.program_id(1)
    rows = slice(j * WINDOW, (j + 1) * WINDOW)
    lane = lax.broadcasted_iota(jnp.int32, (WINDOW, LANES), 1)
    low = lane < HEAD_DIM
    lo_parts, hi_parts = [], []
    for c in range(ATT_WIDTH // LANES):
        blk = q[rows, c * LANES:(c + 1) * LANES]
        lo_parts.append(jnp.where(low, blk, 0.0))
        hi_parts.append(jnp.where(low, 0.0, blk))
    qm = jnp.concatenate(lo_parts + hi_parts, axis=0).astype(BF16)
    kwin = kv_buf[j * WINDOW:j * WINDOW + 3 * WINDOW, :KV_WIDTH]
    vwin = kv_buf[j * WINDOW:j * WINDOW + 3 * WINDOW, KV_WIDTH:]
    s = lax.dot_general(qm, kwin, (((1,), (1,)), ((), ())), preferred_element_type=F32)

    qi = lax.broadcasted_iota(jnp.int32, (WINDOW, 3 * WINDOW), 0)
    ki = lax.broadcasted_iota(jnp.int32, (WINDOW, 3 * WINDOW), 1)
    first_key = t * tm + (j - 1) * WINDOW
    valid = (ki >= qi) & (ki <= qi + 2 * WINDOW) & (ki >= -first_key) & (ki < seq_len - first_key)

    es, rs = [], []
    for hh in range(N_HEADS):
        sh = jnp.where(valid, s[hh * WINDOW:(hh + 1) * WINDOW], MASK_VALUE)
        sk = sink_ref[hh]
        m = jnp.maximum(jnp.max(sh, axis=-1, keepdims=True), sk)
        e = jnp.exp(sh - m)
        den = jnp.sum(e, axis=-1, keepdims=True) + jnp.exp(sk - m)
        es.append(e.astype(BF16))
        rs.append(1.0 / den)
    o = _dot(jnp.concatenate(es, axis=0), vwin)
    for c in range(ATT_WIDTH // LANES):
        o_lo = o[c * WINDOW:(c + 1) * WINDOW] * rs[c]
        o_hi = o[(c + 4) * WINDOW:(c + 5) * WINDOW] * rs[c + 4]
        attn_buf[rows, c * LANES:(c + 1) * LANES] = jnp.where(low, o_lo, o_hi)


def _layer_kernel(last, seq_len, *refs):
    if last:
        (h_ref, p_ref, fm_ref, kvp_ref, kvm_ref, kvn_ref, rq_ref, ln1_ref, wmain_ref, sink_ref,
         wao_ref, wfo_ref, wout_ref, lnpg_ref, wpg_ref, wpe_ref, lnn_ref,
         o_ref, kv_buf, attn_buf) = refs
    else:
        (h_ref, p_ref, fm_ref, kvp_ref, kvm_ref, kvn_ref, rq_ref, ln1_ref, wmain_ref, sink_ref,
         wao_ref, wfo_ref, wout_ref, lnpg_ref, wpg_ref, wpe_ref, lnn_ref, wnext_ref, rk_ref,
         o_ref, kvo_ref, uo_ref, kv_buf, attn_buf) = refs
    tm = h_ref.shape[0]

    h = h_ref[...]
    hn = _rms(h, ln1_ref[...]).astype(BF16)

    kv_buf[:WINDOW] = kvp_ref[...]
    kv_buf[WINDOW:WINDOW + tm] = kvm_ref[...]
    kv_buf[WINDOW + tm:] = kvn_ref[...]
    q = _rope(_dot(hn, wmain_ref[:, _Q0:_GA0]), rq_ref)
    for j in range(tm // WINDOW):
        _attention_block(j, seq_len, q, kv_buf, sink_ref, attn_buf)

    a = (attn_buf[...] * _silu(_dot(hn, wmain_ref[:, _GA0:_GF0]))).astype(BF16)
    f = (fm_ref[...].astype(F32) * _silu(_dot(hn, wmain_ref[:, _GF0:_MGA0]))).astype(BF16)
    merged = (_sigmoid(_dot(hn, wmain_ref[:, _MGA0:_MGF0])) * _dot(a, wao_ref[...])
              + _sigmoid(_dot(hn, wmain_ref[:, _MGF0:_MAIN_W])) * _dot(f, wfo_ref[...]))
    h = h + _dot(merged.astype(BF16), wout_ref[...])

    gate = _sigmoid(_dot(_rms(h, lnpg_ref[...]).astype(BF16), wpg_ref[...]))
    h = h + _dot(p_ref[...].astype(BF16), wpe_ref[...]) * gate

    if last:
        o_ref[...] = _rms(h, lnn_ref[...])
    else:
        o_ref[...] = h
        _project_next(h, lnn_ref, wnext_ref, rk_ref, kvo_ref, uo_ref)


def _layer(i, last, h, p, fm, kv, ropeq, ropek, w):
    nb, S, _ = h.shape
    tm = min(ROW_TILE, S)
    r = tm // WINDOW
    nblk = S // WINDOW
    row = lambda width: pl.BlockSpec((None, tm, width), lambda b, t: (b, t, 0))
    rope_spec = pl.BlockSpec((3, tm, LANES), lambda b, t: (0, t, 0))
    in_specs = [
        row(D_MODEL),
        pl.BlockSpec((None, None, tm, PLE_DIM), lambda b, t: (i, b, t, 0)),
        row(F_WIDTH),
        pl.BlockSpec((None, WINDOW, 2 * KV_WIDTH), lambda b, t: (b, jnp.maximum(t * r - 1, 0), 0)),
        row(2 * KV_WIDTH),
        pl.BlockSpec((None, WINDOW, 2 * KV_WIDTH), lambda b, t: (b, jnp.minimum((t + 1) * r, nblk - 1), 0)),
        rope_spec,
        _const_spec((1, D_MODEL)),
        _const_spec((D_MODEL, _MAIN_W)),
        pl.BlockSpec(memory_space=pltpu.SMEM),
        _const_spec((ATT_WIDTH, D_MODEL)),
        _const_spec((F_WIDTH, D_MODEL)),
        _const_spec((D_MODEL, D_MODEL)),
        _const_spec((1, D_MODEL)),
        _const_spec((D_MODEL, D_MODEL)),
        _const_spec((PLE_DIM, D_MODEL)),
        _const_spec((1, D_MODEL)),
    ]
    args = [h, p, fm, kv, kv, kv, ropeq, w["ln1"], w["wmain"], w["sink"], w["wao"], w["wfo"],
            w["wout"], w["lnpg"], w["wpg"], w["wpe"], w["ln_next"]]
    out_specs = [row(D_MODEL)]
    out_shape = [jax.ShapeDtypeStruct((nb, S, D_MODEL), F32)]
    if not last:
        in_specs += [_const_spec((D_MODEL, NEXT_W)), rope_spec]
        args += [w["wnext"], ropek]
        out_specs += [row(2 * KV_WIDTH), row(F_WIDTH)]
        out_shape += [jax.ShapeDtypeStruct((nb, S, 2 * KV_WIDTH), BF16),
                      jax.ShapeDtypeStruct((nb, S, F_WIDTH), BF16)]
    return pl.pallas_call(
        functools.partial(_layer_kernel, last, S),
        grid=(nb, S // tm),
        in_specs=in_specs,
        out_specs=out_specs,
        out_shape=out_shape,
        scratch_shapes=[pltpu.VMEM((tm + 2 * WINDOW, 2 * KV_WIDTH), BF16),
                        pltpu.VMEM((tm, ATT_WIDTH), F32)],
        compiler_params=pltpu.CompilerParams(
            dimension_semantics=("parallel", "parallel"), vmem_limit_bytes=VMEM_LIMIT),
        name="layer_last" if last else "layer",
    )(*args)


def _head_perm():
    idx = []
    for c in range(ATT_WIDTH // LANES):
        idx += list(range(c * HEAD_DIM, (c + 1) * HEAD_DIM))
        idx += list(range((c + 4) * HEAD_DIM, (c + 5) * HEAD_DIM))
    return np.asarray(idx)


def _prep_weights(ln1, w_in, sink, w_fmix, w_ao, w_fo, w_out, w_pe, ln_pg, w_pg, ln_f):
    perm = _head_perm()
    o = np.cumsum([0, ATT_WIDTH, KV_WIDTH, KV_WIDTH, ATT_WIDTH, F_WIDTH, F_WIDTH, D_MODEL, D_MODEL])
    layers = []
    for i in range(DEPTH):
        wi = w_in[i]
        part = lambda n: wi[:, o[n]:o[n + 1]]
        wmain = jnp.concatenate([part(0)[:, perm], part(3)[:, perm], part(5), part(6), part(7)], axis=1)
        wnext = jnp.concatenate([part(1), part(2), part(4)], axis=1)
        layers.append(dict(
            ln1=ln1[i].reshape(1, D_MODEL), wmain=wmain.astype(BF16), wnext_self=wnext.astype(BF16),
            sink=sink[i], wm=w_fmix[i].astype(BF16), wao=w_ao[i][perm, :].astype(BF16),
            wfo=w_fo[i].astype(BF16), wout=w_out[i].astype(BF16), lnpg=ln_pg[i].reshape(1, D_MODEL),
            wpg=w_pg[i].astype(BF16), wpe=w_pe[i].astype(BF16)))
    for i in range(DEPTH):
        if i + 1 < DEPTH:
            layers[i]["ln_next"] = layers[i + 1]["ln1"]
            layers[i]["wnext"] = layers[i + 1]["wnext_self"]
        else:
            layers[i]["ln_next"] = ln_f.reshape(1, D_MODEL)
    return layers


def _trunk(x, p, layers):
    S = x.shape[1]
    ropeq = jnp.asarray(_rope_tables(S, 1.0 / math.sqrt(HEAD_DIM)))
    ropek = jnp.asarray(_rope_tables(S, 1.0))
    tables = _fft_tables(S)
    kv, u = _prologue(x, layers[0]["ln1"], layers[0]["wnext_self"], ropek)
    h = x
    for i in range(DEPTH):
        w = layers[i]
        fm = _fourier_mix(u, w["wm"], tables)
        last = i + 1 == DEPTH
        outs = _layer(i, last, h, p, fm, kv, ropeq, ropek, w)
        if last:
            h = outs[0]
        else:
            h, kv, u = outs
    return h


def kernel(x_prompt, x_sample, p_prompt, p_sample, ln1, w_in, sink, w_fmix, w_ao, w_fo,
           w_out, w_pe, ln_pg, w_pg, ln_f):
    layers = _prep_weights(ln1, w_in, sink, w_fmix, w_ao, w_fo, w_out, w_pe, ln_pg, w_pg, ln_f)
    return (_trunk(x_prompt, p_prompt, layers), _trunk(x_sample, p_sample, layers))
```

```python
import functools
import math

import numpy as np
import jax
import jax.numpy as jnp
from jax import lax
from jax.experimental import pallas as pl
from jax.experimental.pallas import tpu as pltpu

D_MODEL = 1024
DEPTH = 4
N_HEADS = 8
N_KV_HEADS = 2
HEAD_DIM = 64
ATT_WIDTH = N_HEADS * HEAD_DIM
KV_WIDTH = N_KV_HEADS * HEAD_DIM
WINDOW = 128
ROPE_DIM = 16
ROPE_THETA = 500000.0
F_GROUPS = 4
F_GROUP_DIM = 128
F_WIDTH = F_GROUPS * F_GROUP_DIM
PLE_DIM = 256
EPS = 1e-6
MASK_VALUE = -1e30

F32 = jnp.float32
BF16 = jnp.bfloat16

LANES = 128
ROW_TILE = 512
FFT_FAST = 32
U_ROWS = ROW_TILE // FFT_FAST
FFT1_LANES = 2048
K1_SUB = 8
K1_STEP = 16
VMEM_LIMIT = 56 * 1024 * 1024

_Q0, _GA0, _GF0, _MGA0, _MGF0, _MAIN_W = 0, 512, 1024, 1536, 2560, 3584
NEXT_W = 2 * KV_WIDTH + F_WIDTH


def _dot(a, b):
    return jnp.dot(a, b, preferred_element_type=F32)


def _sigmoid(x):
    return 0.5 * (jnp.tanh(0.5 * x) + 1.0)


def _silu(x):
    return x * _sigmoid(x)


def _rms(x, g):
    ms = jnp.mean(x * x, axis=-1, keepdims=True)
    return x * lax.rsqrt(ms + EPS) * g


def _rope(x, r_ref):
    c, sa, sb = r_ref[0], r_ref[1], r_ref[2]
    cols = []
    for j in range(x.shape[1] // LANES):
        xj = x[:, j * LANES:(j + 1) * LANES]
        cols.append(xj * c
                    + pltpu.roll(xj, ROPE_DIM // 2, 1) * sa
                    + pltpu.roll(xj, LANES - ROPE_DIM // 2, 1) * sb)
    return cols[0] if len(cols) == 1 else jnp.concatenate(cols, axis=1)


@functools.lru_cache(maxsize=None)
def _rope_tables(S, scale):
    half = ROPE_DIM // 2
    inv = ROPE_THETA ** (-np.arange(0, ROPE_DIM, 2, dtype=np.float64) / ROPE_DIM)
    ang = np.arange(S, dtype=np.float64)[:, None] * inv[None, :]
    cos, sin = np.cos(ang), np.sin(ang)
    c = np.ones((S, LANES))
    sa = np.zeros((S, LANES))
    sb = np.zeros((S, LANES))
    for lane in range(LANES):
        m = lane % HEAD_DIM
        if m < half:
            c[:, lane] = cos[:, m]
            sb[:, lane] = -sin[:, m]
        elif m < ROPE_DIM:
            c[:, lane] = cos[:, m - half]
            sa[:, lane] = sin[:, m - half]
    return (np.stack([c, sa, sb]) * scale).astype(np.float32)


@functools.lru_cache(maxsize=None)
def _fft_tables(S):
    B = FFT_FAST
    A = S // B
    k1 = np.arange(A)
    ang1 = 2.0 * np.pi * ((k1[:, None] * k1[None, :]) % A) / A
    f1 = np.concatenate([np.cos(ang1), -np.sin(ang1)], axis=0) / math.sqrt(A)
    k2 = np.arange(B)
    b = np.arange(B)
    kk = k1[:, None, None] + A * k2[None, :, None]
    ang2 = 2.0 * np.pi * ((kk * b[None, None, :]) % S) / S
    gre = (np.cos(ang2) / math.sqrt(B)).reshape(A // K1_SUB, K1_SUB, B, B)
    gim = (-np.sin(ang2) / math.sqrt(B)).reshape(A // K1_SUB, K1_SUB, B, B)
    lb1 = np.zeros((A // K1_SUB, 2, B, K1_SUB, B, K1_SUB))
    lb2 = np.zeros((A // K1_SUB, 2, B, K1_SUB, B, K1_SUB))
    for r in range(K1_SUB):
        lb1[:, 0, :, r, :, r] = gre[:, r]
        lb1[:, 1, :, r, :, r] = gim[:, r]
        lb2[:, 0, :, r, :, r] = -gim[:, r]
        lb2[:, 1, :, r, :, r] = gre[:, r]
    lb1 = lb1.reshape(A // K1_SUB, 2 * B * K1_SUB, B * K1_SUB)
    lb2 = lb2.reshape(A // K1_SUB, 2 * B * K1_SUB, B * K1_SUB)
    c = np.arange(F_GROUP_DIM)
    ang3 = 2.0 * np.pi * ((c[:, None] * c[None, :]) % F_GROUP_DIM) / F_GROUP_DIM
    zero = np.zeros((F_GROUP_DIM, F_GROUP_DIM))
    cc = np.cos(ang3) / math.sqrt(F_GROUP_DIM)
    sc = np.sin(ang3) / math.sqrt(F_GROUP_DIM)
    cc2 = np.block([[cc, zero], [zero, cc]])
    sc2 = np.block([[sc, zero], [zero, sc]])
    return tuple(np.asarray(t, dtype=BF16) for t in (f1, lb1, lb2, cc2, sc2))


@functools.lru_cache(maxsize=None)
def _u_perm_matrix():
    pm = np.zeros((ROW_TILE, ROW_TILE))
    for a in range(U_ROWS):
        for b in range(FFT_FAST):
            pm[b * U_ROWS + a, a * FFT_FAST + b] = 1.0
    return np.asarray(pm, dtype=BF16)


def _const_spec(shape):
    zeros = (0,) * len(shape)
    return pl.BlockSpec(shape, lambda *_: zeros, pipeline_mode=pl.Buffered(1))


def _fourier_kernel(u_ref, f1_ref, lb1_ref, lb2_ref, cc_ref, sc_ref, wm_ref, o_ref, yre_buf, yim_buf):
    A, width = u_ref.shape
    B = FFT_FAST
    j = pl.program_id(1)

    @pl.when(j == 0)
    def _stage1():
        for lc in range(width // FFT1_LANES):
            lanes = slice(lc * FFT1_LANES, (lc + 1) * FFT1_LANES)
            y = _dot(f1_ref[...], u_ref[:, lanes])
            yre_buf[:, lanes] = y[:A].astype(BF16)
            yim_buf[:, lanes] = y[A:].astype(BF16)

    r0 = pl.multiple_of(j * K1_STEP, K1_STEP)
    yre = yre_buf[pl.ds(r0, K1_STEP), :].astype(F32)
    yim = yim_buf[pl.ds(r0, K1_STEP), :].astype(F32)
    half = B * K1_SUB
    zre, zim = [], []
    for s in range(K1_STEP // K1_SUB):
        rows = slice(s * K1_SUB, (s + 1) * K1_SUB)
        tre = jnp.concatenate([yre[rows, b * F_WIDTH:(b + 1) * F_WIDTH] for b in range(B)], axis=0)
        tim = jnp.concatenate([yim[rows, b * F_WIDTH:(b + 1) * F_WIDTH] for b in range(B)], axis=0)
        z = _dot(lb1_ref[s], tre.astype(BF16)) + _dot(lb2_ref[s], tim.astype(BF16))
        zre.append(z[:half])
        zim.append(z[half:])
    order = lambda parts: jnp.concatenate(
        [p[k2 * K1_SUB:(k2 + 1) * K1_SUB] for k2 in range(B) for p in parts], axis=0).astype(BF16)
    zre = order(zre)
    zim = order(zim)
    pair = 2 * F_GROUP_DIM
    for g in range(F_GROUPS // 2):
        lanes = slice(g * pair, (g + 1) * pair)
        r = _dot(zre[:, lanes], cc_ref[...]) + _dot(zim[:, lanes], sc_ref[...])
        y = _dot(r.astype(BF16), wm_ref[g]).astype(BF16)
        for k2 in range(B):
            o_ref[k2, :, lanes] = y[k2 * K1_STEP:(k2 + 1) * K1_STEP]


def _fourier_mix(u1, wm2, tables):
    f1, lb1, lb2, cc2, sc2 = tables
    nb, A, width = u1.shape
    B = FFT_FAST
    nsub = K1_STEP // K1_SUB
    lspec = pl.BlockSpec((nsub, 2 * B * K1_SUB, B * K1_SUB), lambda i, j: (j, 0, 0))
    pair = 2 * F_GROUP_DIM
    out = pl.pallas_call(
        _fourier_kernel,
        grid=(nb, A // K1_STEP),
        in_specs=[pl.BlockSpec((None, A, width), lambda i, j: (i, 0, 0)),
                  _const_spec((2 * A, A)), lspec, lspec,
                  _const_spec((pair, pair)), _const_spec((pair, pair)),
                  _const_spec((F_GROUPS // 2, pair, pair))],
        out_specs=pl.BlockSpec((None, B, K1_STEP, F_WIDTH), lambda i, j: (i, 0, j, 0)),
        out_shape=jax.ShapeDtypeStruct((nb, B, A, F_WIDTH), BF16),
        scratch_shapes=[pltpu.VMEM((A, width), BF16), pltpu.VMEM((A, width), BF16)],
        compiler_params=pltpu.CompilerParams(
            dimension_semantics=("parallel", "arbitrary"), vmem_limit_bytes=VMEM_LIMIT),
        name="fourier",
    )(u1, f1, lb1, lb2, cc2, sc2, wm2)
    return out.reshape(nb, B * A, F_WIDTH)


def _project_next(h, ln_ref, wnext_ref, rk_ref, pm_ref, kv_ref, u_ref):
    hn = _rms(h, ln_ref[...]).astype(BF16)
    z = _dot(hn, wnext_ref[...])
    k = _rope(z[:, :KV_WIDTH], rk_ref)
    kv_ref[:, :KV_WIDTH] = k.astype(BF16)
    kv_ref[:, KV_WIDTH:] = z[:, KV_WIDTH:2 * KV_WIDTH].astype(BF16)
    u = _dot(pm_ref[...], z[:, 2 * KV_WIDTH:].astype(BF16)).astype(BF16)
    for b in range(FFT_FAST):
        u_ref[:, b * F_WIDTH:(b + 1) * F_WIDTH] = u[b * U_ROWS:(b + 1) * U_ROWS]


def _prologue_kernel(x_ref, ln_ref, wnext_ref, rk_ref, pm_ref, kv_ref, u_ref):
    _project_next(x_ref[...], ln_ref, wnext_ref, rk_ref, pm_ref, kv_ref, u_ref)


def _u_spec():
    return pl.BlockSpec((None, U_ROWS, FFT_FAST * F_WIDTH), lambda b, t: (b, t, 0))


def _u_shape(nb, S):
    return jax.ShapeDtypeStruct((nb, S // FFT_FAST, FFT_FAST * F_WIDTH), BF16)


def _prologue(x, ln, wnext, ropek, pm):
    nb, S, _ = x.shape
    tm = ROW_TILE
    row = lambda w: pl.BlockSpec((None, tm, w), lambda b, t: (b, t, 0))
    return pl.pallas_call(
        _prologue_kernel,
        grid=(nb, S // tm),
        in_specs=[row(D_MODEL), _const_spec((1, D_MODEL)), _const_spec((D_MODEL, NEXT_W)),
                  pl.BlockSpec((3, tm, LANES), lambda b, t: (0, t, 0)), _const_spec((tm, tm))],
        out_specs=[row(2 * KV_WIDTH), _u_spec()],
        out_shape=[jax.ShapeDtypeStruct((nb, S, 2 * KV_WIDTH), BF16), _u_shape(nb, S)],
        compiler_params=pltpu.CompilerParams(
            dimension_semantics=("parallel", "parallel"), vmem_limit_bytes=VMEM_LIMIT),
        name="prologue",
    )(x, ln, wnext, ropek, pm)


def _attention_block(j, seq_len, q, kv_buf, sink_ref, attn_buf):
    tm = q.shape[0]
    t = pl.program_id(1)
    rows = slice(j * WINDOW, (j + 1) * WINDOW)
    lane = lax.broadcasted_iota(jnp.int32, (WINDOW, LANES), 1)
    low = lane < HEAD_DIM
    lo_parts, hi_parts = [], []
    for c in range(ATT_WIDTH // LANES):
        blk = q[rows, c * LANES:(c + 1) * LANES]
        lo_parts.append(jnp.where(low, blk, 0.0))
        hi_parts.append(jnp.where(low, 0.0, blk))
    qm = jnp.concatenate(lo_parts + hi_parts, axis=0).astype(BF16)
    kwin = kv_buf[j * WINDOW:j * WINDOW + 3 * WINDOW, :KV_WIDTH]
    vwin = kv_buf[j * WINDOW:j * WINDOW + 3 * WINDOW, KV_WIDTH:]
    s = lax.dot_general(qm, kwin, (((1,), (1,)), ((), ())), preferred_element_type=F32)

    qi = lax.broadcasted_iota(jnp.int32, (WINDOW, 3 * WINDOW), 0)
    ki = lax.broadcasted_iota(jnp.int32, (WINDOW, 3 * WINDOW), 1)
    first_key = t * tm + (j - 1) * WINDOW
    valid = (ki >= qi) & (ki <= qi + 2 * WINDOW) & (ki >= -first_key) & (ki < seq_len - first_key)

    es, rs = [], []
    for hh in range(N_HEADS):
        sh = jnp.where(valid, s[hh * WINDOW:(hh + 1) * WINDOW], MASK_VALUE)
        sk = sink_ref[hh]
        m = jnp.maximum(jnp.max(sh, axis=-1, keepdims=True), sk)
        e = jnp.exp(sh - m)
        den = jnp.sum(e, axis=-1, keepdims=True) + jnp.exp(sk - m)
        es.append(e.astype(BF16))
        rs.append(1.0 / den)
    o = _dot(jnp.concatenate(es, axis=0), vwin)
    for c in range(ATT_WIDTH // LANES):
        o_lo = o[c * WINDOW:(c + 1) * WINDOW] * rs[c]
        o_hi = o[(c + 4) * WINDOW:(c + 5) * WINDOW] * rs[c + 4]
        attn_buf[rows, c * LANES:(c + 1) * LANES] = jnp.where(low, o_lo, o_hi)


def _layer_kernel(last, seq_len, *refs):
    if last:
        (h_ref, p_ref, fm_ref, kvp_ref, kvm_ref, kvn_ref, rq_ref, ln1_ref, wmain_ref, sink_ref,
         wao_ref, wfo_ref, wout_ref, lnpg_ref, wpg_ref, wpe_ref, lnn_ref,
         o_ref, kv_buf, attn_buf) = refs
    else:
        (h_ref, p_ref, fm_ref, kvp_ref, kvm_ref, kvn_ref, rq_ref, ln1_ref, wmain_ref, sink_ref,
         wao_ref, wfo_ref, wout_ref, lnpg_ref, wpg_ref, wpe_ref, lnn_ref, wnext_ref, rk_ref, pm_ref,
         o_ref, kvo_ref, uo_ref, kv_buf, attn_buf) = refs
    tm = h_ref.shape[0]

    h = h_ref[...]
    hn = _rms(h, ln1_ref[...]).astype(BF16)

    kv_buf[:WINDOW] = kvp_ref[...]
    kv_buf[WINDOW:WINDOW + tm] = kvm_ref[...]
    kv_buf[WINDOW + tm:] = kvn_ref[...]
    q = _rope(_dot(hn, wmain_ref[:, _Q0:_GA0]), rq_ref)
    for j in range(tm // WINDOW):
        _attention_block(j, seq_len, q, kv_buf, sink_ref, attn_buf)

    a = (attn_buf[...] * _silu(_dot(hn, wmain_ref[:, _GA0:_GF0]))).astype(BF16)
    f = (fm_ref[...].astype(F32) * _silu(_dot(hn, wmain_ref[:, _GF0:_MGA0]))).astype(BF16)
    merged = (_sigmoid(_dot(hn, wmain_ref[:, _MGA0:_MGF0])) * _dot(a, wao_ref[...])
              + _sigmoid(_dot(hn, wmain_ref[:, _MGF0:_MAIN_W])) * _dot(f, wfo_ref[...]))
    h = h + _dot(merged.astype(BF16), wout_ref[...])

    gate = _sigmoid(_dot(_rms(h, lnpg_ref[...]).astype(BF16), wpg_ref[...]))
    h = h + _dot(p_ref[...].astype(BF16), wpe_ref[...]) * gate

    if last:
        o_ref[...] = _rms(h, lnn_ref[...])
    else:
        o_ref[...] = h
        _project_next(h, lnn_ref, wnext_ref, rk_ref, pm_ref, kvo_ref, uo_ref)


def _layer(i, last, h, p, fm, kv, ropeq, ropek, pm, w):
    nb, S, _ = h.shape
    tm = ROW_TILE
    r = tm // WINDOW
    nblk = S // WINDOW
    row = lambda width: pl.BlockSpec((None, tm, width), lambda b, t: (b, t, 0))
    rope_spec = pl.BlockSpec((3, tm, LANES), lambda b, t: (0, t, 0))
    in_specs = [
        row(D_MODEL),
        pl.BlockSpec((None, None, tm, PLE_DIM), lambda b, t: (i, b, t, 0)),
        row(F_WIDTH),
        pl.BlockSpec((None, WINDOW, 2 * KV_WIDTH), lambda b, t: (b, jnp.maximum(t * r - 1, 0), 0)),
        row(2 * KV_WIDTH),
        pl.BlockSpec((None, WINDOW, 2 * KV_WIDTH), lambda b, t: (b, jnp.minimum((t + 1) * r, nblk - 1), 0)),
        rope_spec,
        _const_spec((1, D_MODEL)),
        _const_spec((D_MODEL, _MAIN_W)),
        pl.BlockSpec(memory_space=pltpu.SMEM),
        _const_spec((ATT_WIDTH, D_MODEL)),
        _const_spec((F_WIDTH, D_MODEL)),
        _const_spec((D_MODEL, D_MODEL)),
        _const_spec((1, D_MODEL)),
        _const_spec((D_MODEL, D_MODEL)),
        _const_spec((PLE_DIM, D_MODEL)),
        _const_spec((1, D_MODEL)),
    ]
    args = [h, p, fm, kv, kv, kv, ropeq, w["ln1"], w["wmain"], w["sink"], w["wao"], w["wfo"],
            w["wout"], w["lnpg"], w["wpg"], w["wpe"], w["ln_next"]]
    out_specs = [row(D_MODEL)]
    out_shape = [jax.ShapeDtypeStruct((nb, S, D_MODEL), F32)]
    if not last:
        in_specs += [_const_spec((D_MODEL, NEXT_W)), rope_spec, _const_spec((tm, tm))]
        args += [w["wnext"], ropek, pm]
        out_specs += [row(2 * KV_WIDTH), _u_spec()]
        out_shape += [jax.ShapeDtypeStruct((nb, S, 2 * KV_WIDTH), BF16), _u_shape(nb, S)]
    return pl.pallas_call(
        functools.partial(_layer_kernel, last, S),
        grid=(nb, S // tm),
        in_specs=in_specs,
        out_specs=out_specs,
        out_shape=out_shape,
        scratch_shapes=[pltpu.VMEM((tm + 2 * WINDOW, 2 * KV_WIDTH), BF16),
                        pltpu.VMEM((tm, ATT_WIDTH), F32)],
        compiler_params=pltpu.CompilerParams(
            dimension_semantics=("parallel", "parallel"), vmem_limit_bytes=VMEM_LIMIT),
        name="layer_last" if last else "layer",
    )(*args)


def _head_perm():
    idx = []
    for c in range(ATT_WIDTH // LANES):
        idx += list(range(c * HEAD_DIM, (c + 1) * HEAD_DIM))
        idx += list(range((c + 4) * HEAD_DIM, (c + 5) * HEAD_DIM))
    return np.asarray(idx)


def _prep_weights(ln1, w_in, sink, w_fmix, w_ao, w_fo, w_out, w_pe, ln_pg, w_pg, ln_f):
    perm = _head_perm()
    o = np.cumsum([0, ATT_WIDTH, KV_WIDTH, KV_WIDTH, ATT_WIDTH, F_WIDTH, F_WIDTH, D_MODEL, D_MODEL])
    layers = []
    for i in range(DEPTH):
        wi = w_in[i]
        part = lambda n: wi[:, o[n]:o[n + 1]]
        wmain = jnp.concatenate([part(0)[:, perm], part(3)[:, perm], part(5), part(6), part(7)], axis=1)
        wnext = jnp.concatenate([part(1), part(2), part(4)], axis=1)
        zero = jnp.zeros((F_GROUP_DIM, F_GROUP_DIM), F32)
        wm2 = jnp.stack([jnp.block([[w_fmix[i, 2 * g], zero], [zero, w_fmix[i, 2 * g + 1]]])
                         for g in range(F_GROUPS // 2)])
        layers.append(dict(
            ln1=ln1[i].reshape(1, D_MODEL), wmain=wmain.astype(BF16), wnext_self=wnext.astype(BF16),
            sink=sink[i], wm=wm2.astype(BF16), wao=w_ao[i][perm, :].astype(BF16),
            wfo=w_fo[i].astype(BF16), wout=w_out[i].astype(BF16), lnpg=ln_pg[i].reshape(1, D_MODEL),
            wpg=w_pg[i].astype(BF16), wpe=w_pe[i].astype(BF16)))
    for i in range(DEPTH):
        if i + 1 < DEPTH:
            layers[i]["ln_next"] = layers[i + 1]["ln1"]
            layers[i]["wnext"] = layers[i + 1]["wnext_self"]
        else:
            layers[i]["ln_next"] = ln_f.reshape(1, D_MODEL)
    return layers


def _trunk(x, p, layers):
    S = x.shape[1]
    ropeq = jnp.asarray(_rope_tables(S, 1.0 / math.sqrt(HEAD_DIM)))
    ropek = jnp.asarray(_rope_tables(S, 1.0))
    tables = _fft_tables(S)
    pm = _u_perm_matrix()
    kv, u = _prologue(x, layers[0]["ln1"], layers[0]["wnext_self"], ropek, pm)
    h = x
    for i in range(DEPTH):
        w = layers[i]
        fm = _fourier_mix(u, w["wm"], tables)
        last = i + 1 == DEPTH
        outs = _layer(i, last, h, p, fm, kv, ropeq, ropek, pm, w)
        if last:
            h = outs[0]
        else:
            h, kv, u = outs
    return h


def kernel(x_prompt, x_sample, p_prompt, p_sample, ln1, w_in, sink, w_fmix, w_ao, w_fo,
           w_out, w_pe, ln_pg, w_pg, ln_f):
    layers = _prep_weights(ln1, w_in, sink, w_fmix, w_ao, w_fo, w_out, w_pe, ln_pg, w_pg, ln_f)
    return (_trunk(x_prompt, p_prompt, layers), _trunk(x_sample, p_sample, layers))
```

```python
import functools
import math

import numpy as np
import jax
import jax.numpy as jnp
from jax import lax
from jax.experimental import pallas as pl
from jax.experimental.pallas import tpu as pltpu

D_MODEL = 1024
DEPTH = 4
N_HEADS = 8
N_KV_HEADS = 2
HEAD_DIM = 64
ATT_WIDTH = N_HEADS * HEAD_DIM
KV_WIDTH = N_KV_HEADS * HEAD_DIM
WINDOW = 128
ROPE_DIM = 16
ROPE_THETA = 500000.0
F_GROUPS = 4
F_GROUP_DIM = 128
F_WIDTH = F_GROUPS * F_GROUP_DIM
PLE_DIM = 256
EPS = 1e-6
MASK_VALUE = -1e30
LOG2E = math.log2(math.e)

F32 = jnp.float32
BF16 = jnp.bfloat16

LANES = 128
ROW_TILE = 512
FFT_FAST = 32
U_ROWS = ROW_TILE // FFT_FAST
FFT1_LANES = 2048
K1_SUB = 8
K1_STEP = 16
VMEM_LIMIT = 56 * 1024 * 1024

_Q0, _GA0, _GF0, _MGA0, _MGF0, _MAIN_W = 0, 512, 1024, 1536, 2560, 3584
NEXT_W = 2 * KV_WIDTH + F_WIDTH


def _dot(a, b):
    return jnp.dot(a, b, preferred_element_type=F32)


def _sigmoid(x):
    return 0.5 * (jnp.tanh(0.5 * x) + 1.0)


def _silu(x):
    return x * _sigmoid(x)


def _rms(x, g):
    ms = jnp.mean(x * x, axis=-1, keepdims=True)
    return x * lax.rsqrt(ms + EPS) * g


def _rope(x, r_ref):
    c, sa, sb = r_ref[0], r_ref[1], r_ref[2]
    cols = []
    for j in range(x.shape[1] // LANES):
        xj = x[:, j * LANES:(j + 1) * LANES]
        cols.append(xj * c
                    + pltpu.roll(xj, ROPE_DIM // 2, 1) * sa
                    + pltpu.roll(xj, LANES - ROPE_DIM // 2, 1) * sb)
    return cols[0] if len(cols) == 1 else jnp.concatenate(cols, axis=1)


@functools.lru_cache(maxsize=None)
def _rope_tables(S, scale):
    half = ROPE_DIM // 2
    inv = ROPE_THETA ** (-np.arange(0, ROPE_DIM, 2, dtype=np.float64) / ROPE_DIM)
    ang = np.arange(S, dtype=np.float64)[:, None] * inv[None, :]
    cos, sin = np.cos(ang), np.sin(ang)
    c = np.ones((S, LANES))
    sa = np.zeros((S, LANES))
    sb = np.zeros((S, LANES))
    for lane in range(LANES):
        m = lane % HEAD_DIM
        if m < half:
            c[:, lane] = cos[:, m]
            sb[:, lane] = -sin[:, m]
        elif m < ROPE_DIM:
            c[:, lane] = cos[:, m - half]
            sa[:, lane] = sin[:, m - half]
    return (np.stack([c, sa, sb]) * scale).astype(np.float32)


@functools.lru_cache(maxsize=None)
def _fft_tables(S):
    B = FFT_FAST
    A = S // B
    k1 = np.arange(A)
    ang1 = 2.0 * np.pi * ((k1[:, None] * k1[None, :]) % A) / A
    f1 = np.concatenate([np.cos(ang1), -np.sin(ang1)], axis=0) / math.sqrt(A)
    k2 = np.arange(B)
    b = np.arange(B)
    kk = k1[:, None, None] + A * k2[None, :, None]
    ang2 = 2.0 * np.pi * ((kk * b[None, None, :]) % S) / S
    gre = (np.cos(ang2) / math.sqrt(B)).reshape(A // K1_SUB, K1_SUB, B, B)
    gim = (-np.sin(ang2) / math.sqrt(B)).reshape(A // K1_SUB, K1_SUB, B, B)
    lb1 = np.zeros((A // K1_SUB, 2, B, K1_SUB, B, K1_SUB))
    lb2 = np.zeros((A // K1_SUB, 2, B, K1_SUB, B, K1_SUB))
    for r in range(K1_SUB):
        lb1[:, 0, :, r, :, r] = gre[:, r]
        lb1[:, 1, :, r, :, r] = gim[:, r]
        lb2[:, 0, :, r, :, r] = -gim[:, r]
        lb2[:, 1, :, r, :, r] = gre[:, r]
    lb1 = lb1.reshape(A // K1_SUB, 2 * B * K1_SUB, B * K1_SUB)
    lb2 = lb2.reshape(A // K1_SUB, 2 * B * K1_SUB, B * K1_SUB)
    c = np.arange(F_GROUP_DIM)
    ang3 = 2.0 * np.pi * ((c[:, None] * c[None, :]) % F_GROUP_DIM) / F_GROUP_DIM
    zero = np.zeros((F_GROUP_DIM, F_GROUP_DIM))
    cc = np.cos(ang3) / math.sqrt(F_GROUP_DIM)
    sc = np.sin(ang3) / math.sqrt(F_GROUP_DIM)
    cc2 = np.block([[cc, zero], [zero, cc]])
    sc2 = np.block([[sc, zero], [zero, sc]])
    return tuple(np.asarray(t, dtype=BF16) for t in (f1, lb1, lb2, cc2, sc2))


@functools.lru_cache(maxsize=None)
def _u_perm_matrix():
    pm = np.zeros((ROW_TILE, ROW_TILE))
    for a in range(U_ROWS):
        for b in range(FFT_FAST):
            pm[b * U_ROWS + a, a * FFT_FAST + b] = 1.0
    return np.asarray(pm, dtype=BF16)


def _const_spec(shape):
    zeros = (0,) * len(shape)
    return pl.BlockSpec(shape, lambda *_: zeros, pipeline_mode=pl.Buffered(1))


def _fourier_kernel(u_ref, f1_ref, lb1_ref, lb2_ref, cc_ref, sc_ref, wm_ref, o_ref, yre_buf, yim_buf):
    A, width = u_ref.shape
    B = FFT_FAST
    j = pl.program_id(1)

    @pl.when(j == 0)
    def _stage1():
        for lc in range(width // FFT1_LANES):
            lanes = slice(lc * FFT1_LANES, (lc + 1) * FFT1_LANES)
            y = _dot(f1_ref[...], u_ref[:, lanes])
            yre_buf[:, lanes] = y[:A].astype(BF16)
            yim_buf[:, lanes] = y[A:].astype(BF16)

    r0 = pl.multiple_of(j * K1_STEP, K1_STEP)
    yre = yre_buf[pl.ds(r0, K1_STEP), :].astype(F32)
    yim = yim_buf[pl.ds(r0, K1_STEP), :].astype(F32)
    half = B * K1_SUB
    zre, zim = [], []
    for s in range(K1_STEP // K1_SUB):
        rows = slice(s * K1_SUB, (s + 1) * K1_SUB)
        tre = jnp.concatenate([yre[rows, b * F_WIDTH:(b + 1) * F_WIDTH] for b in range(B)], axis=0)
        tim = jnp.concatenate([yim[rows, b * F_WIDTH:(b + 1) * F_WIDTH] for b in range(B)], axis=0)
        z = _dot(lb1_ref[s], tre.astype(BF16)) + _dot(lb2_ref[s], tim.astype(BF16))
        zre.append(z[:half])
        zim.append(z[half:])
    order = lambda parts: jnp.concatenate(
        [p[k2 * K1_SUB:(k2 + 1) * K1_SUB] for k2 in range(B) for p in parts], axis=0).astype(BF16)
    zre = order(zre)
    zim = order(zim)
    pair = 2 * F_GROUP_DIM
    for g in range(F_GROUPS // 2):
        lanes = slice(g * pair, (g + 1) * pair)
        r = _dot(zre[:, lanes], cc_ref[...]) + _dot(zim[:, lanes], sc_ref[...])
        y = _dot(r.astype(BF16), wm_ref[g]).astype(BF16)
        for k2 in range(B):
            o_ref[k2, :, lanes] = y[k2 * K1_STEP:(k2 + 1) * K1_STEP]


def _fourier_mix(u1, wm2, tables):
    f1, lb1, lb2, cc2, sc2 = tables
    nb, A, width = u1.shape
    B = FFT_FAST
    nsub = K1_STEP // K1_SUB
    lspec = pl.BlockSpec((nsub, 2 * B * K1_SUB, B * K1_SUB), lambda i, j: (j, 0, 0))
    pair = 2 * F_GROUP_DIM
    out = pl.pallas_call(
        _fourier_kernel,
        grid=(nb, A // K1_STEP),
        in_specs=[pl.BlockSpec((None, A, width), lambda i, j: (i, 0, 0)),
                  _const_spec((2 * A, A)), lspec, lspec,
                  _const_spec((pair, pair)), _const_spec((pair, pair)),
                  _const_spec((F_GROUPS // 2, pair, pair))],
        out_specs=pl.BlockSpec((None, B, K1_STEP, F_WIDTH), lambda i, j: (i, 0, j, 0)),
        out_shape=jax.ShapeDtypeStruct((nb, B, A, F_WIDTH), BF16),
        scratch_shapes=[pltpu.VMEM((A, width), BF16), pltpu.VMEM((A, width), BF16)],
        compiler_params=pltpu.CompilerParams(
            dimension_semantics=("parallel", "arbitrary"), vmem_limit_bytes=VMEM_LIMIT),
        name="fourier",
    )(u1, f1, lb1, lb2, cc2, sc2, wm2)
    return out.reshape(nb, B * A, F_WIDTH)


def _project_next(h, ln_ref, wnext_ref, rk_ref, pm_ref, kv_ref, u_ref):
    hn = _rms(h, ln_ref[...]).astype(BF16)
    z = _dot(hn, wnext_ref[...])
    k = _rope(z[:, :KV_WIDTH], rk_ref)
    kv_ref[:, :KV_WIDTH] = k.astype(BF16)
    kv_ref[:, KV_WIDTH:] = z[:, KV_WIDTH:2 * KV_WIDTH].astype(BF16)
    u = _dot(pm_ref[...], z[:, 2 * KV_WIDTH:].astype(BF16)).astype(BF16)
    for b in range(FFT_FAST):
        u_ref[:, b * F_WIDTH:(b + 1) * F_WIDTH] = u[b * U_ROWS:(b + 1) * U_ROWS]


def _prologue_kernel(x_ref, ln_ref, wnext_ref, rk_ref, pm_ref, kv_ref, u_ref):
    _project_next(x_ref[...], ln_ref, wnext_ref, rk_ref, pm_ref, kv_ref, u_ref)


def _u_spec():
    return pl.BlockSpec((None, U_ROWS, FFT_FAST * F_WIDTH), lambda b, t: (b, t, 0))


def _u_shape(nb, S):
    return jax.ShapeDtypeStruct((nb, S // FFT_FAST, FFT_FAST * F_WIDTH), BF16)


def _prologue(x, ln, wnext, ropek, pm):
    nb, S, _ = x.shape
    tm = ROW_TILE
    row = lambda w: pl.BlockSpec((None, tm, w), lambda b, t: (b, t, 0))
    return pl.pallas_call(
        _prologue_kernel,
        grid=(nb, S // tm),
        in_specs=[row(D_MODEL), _const_spec((1, D_MODEL)), _const_spec((D_MODEL, NEXT_W)),
                  pl.BlockSpec((3, tm, LANES), lambda b, t: (0, t, 0)), _const_spec((tm, tm))],
        out_specs=[row(2 * KV_WIDTH), _u_spec()],
        out_shape=[jax.ShapeDtypeStruct((nb, S, 2 * KV_WIDTH), BF16), _u_shape(nb, S)],
        compiler_params=pltpu.CompilerParams(
            dimension_semantics=("parallel", "parallel"), vmem_limit_bytes=VMEM_LIMIT),
        name="prologue",
    )(x, ln, wnext, ropek, pm)


def _attention_block(j, seq_len, q, kv_buf, sink_ref, attn_buf):
    tm = q.shape[0]
    t = pl.program_id(1)
    rows = slice(j * WINDOW, (j + 1) * WINDOW)
    lane = lax.broadcasted_iota(jnp.int32, (WINDOW, LANES), 1)
    low = lane < HEAD_DIM
    lo_parts, hi_parts = [], []
    for c in range(ATT_WIDTH // LANES):
        blk = q[rows, c * LANES:(c + 1) * LANES]
        lo_parts.append(jnp.where(low, blk, 0.0))
        hi_parts.append(jnp.where(low, 0.0, blk))
    qm = jnp.concatenate(lo_parts + hi_parts, axis=0).astype(BF16)
    kwin = kv_buf[j * WINDOW:j * WINDOW + 3 * WINDOW, :KV_WIDTH]
    vwin = kv_buf[j * WINDOW:j * WINDOW + 3 * WINDOW, KV_WIDTH:]
    s = lax.dot_general(qm, kwin, (((1,), (1,)), ((), ())), preferred_element_type=F32)

    qi = lax.broadcasted_iota(jnp.int32, (WINDOW, WINDOW), 0)
    ki = lax.broadcasted_iota(jnp.int32, (WINDOW, WINDOW), 1)
    first_key = t * tm + (j - 1) * WINDOW
    valid_lo = (ki >= qi) & (first_key >= 0)
    valid_hi = (ki <= qi) & (first_key + 2 * WINDOW < seq_len)

    es, rs = [], []
    for hh in range(N_HEADS):
        sh = s[hh * WINDOW:(hh + 1) * WINDOW]
        s0 = jnp.where(valid_lo, sh[:, :WINDOW], MASK_VALUE)
        s1 = sh[:, WINDOW:2 * WINDOW]
        s2 = jnp.where(valid_hi, sh[:, 2 * WINDOW:], MASK_VALUE)
        sk = sink_ref[hh] * LOG2E
        m = jnp.maximum(jnp.max(jnp.maximum(jnp.maximum(s0, s1), s2), axis=-1, keepdims=True), sk)
        e0, e1, e2 = jnp.exp2(s0 - m), jnp.exp2(s1 - m), jnp.exp2(s2 - m)
        den = jnp.sum(e0 + e1 + e2, axis=-1, keepdims=True) + jnp.exp2(sk - m)
        es.append(jnp.concatenate([e0, e1, e2], axis=1).astype(BF16))
        rs.append(1.0 / den)
    o = _dot(jnp.concatenate(es, axis=0), vwin)
    for c in range(ATT_WIDTH // LANES):
        o_lo = o[c * WINDOW:(c + 1) * WINDOW] * rs[c]
        o_hi = o[(c + 4) * WINDOW:(c + 5) * WINDOW] * rs[c + 4]
        attn_buf[rows, c * LANES:(c + 1) * LANES] = jnp.where(low, o_lo, o_hi)


def _layer_kernel(last, seq_len, *refs):
    if last:
        (h_ref, p_ref, fm_ref, kvp_ref, kvm_ref, kvn_ref, rq_ref, ln1_ref, wmain_ref, sink_ref,
         wao_ref, wfo_ref, wout_ref, lnpg_ref, wpg_ref, wpe_ref, lnn_ref,
         o_ref, kv_buf, attn_buf) = refs
    else:
        (h_ref, p_ref, fm_ref, kvp_ref, kvm_ref, kvn_ref, rq_ref, ln1_ref, wmain_ref, sink_ref,
         wao_ref, wfo_ref, wout_ref, lnpg_ref, wpg_ref, wpe_ref, lnn_ref, wnext_ref, rk_ref, pm_ref,
         o_ref, kvo_ref, uo_ref, kv_buf, attn_buf) = refs
    tm = h_ref.shape[0]

    h = h_ref[...]
    hn = _rms(h, ln1_ref[...]).astype(BF16)

    kv_buf[:WINDOW] = kvp_ref[...]
    kv_buf[WINDOW:WINDOW + tm] = kvm_ref[...]
    kv_buf[WINDOW + tm:] = kvn_ref[...]
    q = _rope(_dot(hn, wmain_ref[:, _Q0:_GA0]), rq_ref)
    f = (fm_ref[...].astype(F32) * _silu(_dot(hn, wmain_ref[:, _GF0:_MGA0]))).astype(BF16)
    pf = _sigmoid(_dot(hn, wmain_ref[:, _MGF0:_MAIN_W])) * _dot(f, wfo_ref[...])
    ga = _silu(_dot(hn, wmain_ref[:, _GA0:_GF0]))
    mga = _sigmoid(_dot(hn, wmain_ref[:, _MGA0:_MGF0]))
    for j in range(tm // WINDOW):
        _attention_block(j, seq_len, q, kv_buf, sink_ref, attn_buf)

    a = (attn_buf[...] * ga).astype(BF16)
    merged = mga * _dot(a, wao_ref[...]) + pf
    h = h + _dot(merged.astype(BF16), wout_ref[...])

    gate = _sigmoid(_dot(_rms(h, lnpg_ref[...]).astype(BF16), wpg_ref[...]))
    h = h + _dot(p_ref[...].astype(BF16), wpe_ref[...]) * gate

    if last:
        o_ref[...] = _rms(h, lnn_ref[...])
    else:
        o_ref[...] = h
        _project_next(h, lnn_ref, wnext_ref, rk_ref, pm_ref, kvo_ref, uo_ref)


def _layer(i, last, h, p, fm, kv, ropeq, ropek, pm, w):
    nb, S, _ = h.shape
    tm = ROW_TILE
    r = tm // WINDOW
    nblk = S // WINDOW
    row = lambda width: pl.BlockSpec((None, tm, width), lambda b, t: (b, t, 0))
    rope_spec = pl.BlockSpec((3, tm, LANES), lambda b, t: (0, t, 0))
    in_specs = [
        row(D_MODEL),
        pl.BlockSpec((None, None, tm, PLE_DIM), lambda b, t: (i, b, t, 0)),
        row(F_WIDTH),
        pl.BlockSpec((None, WINDOW, 2 * KV_WIDTH), lambda b, t: (b, jnp.maximum(t * r - 1, 0), 0)),
        row(2 * KV_WIDTH),
        pl.BlockSpec((None, WINDOW, 2 * KV_WIDTH), lambda b, t: (b, jnp.minimum((t + 1) * r, nblk - 1), 0)),
        rope_spec,
        _const_spec((1, D_MODEL)),
        _const_spec((D_MODEL, _MAIN_W)),
        pl.BlockSpec(memory_space=pltpu.SMEM),
        _const_spec((ATT_WIDTH, D_MODEL)),
        _const_spec((F_WIDTH, D_MODEL)),
        _const_spec((D_MODEL, D_MODEL)),
        _const_spec((1, D_MODEL)),
        _const_spec((D_MODEL, D_MODEL)),
        _const_spec((PLE_DIM, D_MODEL)),
        _const_spec((1, D_MODEL)),
    ]
    args = [h, p, fm, kv, kv, kv, ropeq, w["ln1"], w["wmain"], w["sink"], w["wao"], w["wfo"],
            w["wout"], w["lnpg"], w["wpg"], w["wpe"], w["ln_next"]]
    out_specs = [row(D_MODEL)]
    out_shape = [jax.ShapeDtypeStruct((nb, S, D_MODEL), F32)]
    if not last:
        in_specs += [_const_spec((D_MODEL, NEXT_W)), rope_spec, _const_spec((tm, tm))]
        args += [w["wnext"], ropek, pm]
        out_specs += [row(2 * KV_WIDTH), _u_spec()]
        out_shape += [jax.ShapeDtypeStruct((nb, S, 2 * KV_WIDTH), BF16), _u_shape(nb, S)]
    return pl.pallas_call(
        functools.partial(_layer_kernel, last, S),
        grid=(nb, S // tm),
        in_specs=in_specs,
        out_specs=out_specs,
        out_shape=out_shape,
        scratch_shapes=[pltpu.VMEM((tm + 2 * WINDOW, 2 * KV_WIDTH), BF16),
                        pltpu.VMEM((tm, ATT_WIDTH), F32)],
        compiler_params=pltpu.CompilerParams(
            dimension_semantics=("parallel", "parallel"), vmem_limit_bytes=VMEM_LIMIT),
        name="layer_last" if last else "layer",
    )(*args)


def _head_perm():
    idx = []
    for c in range(ATT_WIDTH // LANES):
        idx += list(range(c * HEAD_DIM, (c + 1) * HEAD_DIM))
        idx += list(range((c + 4) * HEAD_DIM, (c + 5) * HEAD_DIM))
    return np.asarray(idx)


def _prep_weights(ln1, w_in, sink, w_fmix, w_ao, w_fo, w_out, w_pe, ln_pg, w_pg, ln_f):
    perm = _head_perm()
    o = np.cumsum([0, ATT_WIDTH, KV_WIDTH, KV_WIDTH, ATT_WIDTH, F_WIDTH, F_WIDTH, D_MODEL, D_MODEL])
    layers = []
    for i in range(DEPTH):
        wi = w_in[i]
        part = lambda n: wi[:, o[n]:o[n + 1]]
        wmain = jnp.concatenate([part(0)[:, perm], part(3)[:, perm], part(5), part(6), part(7)], axis=1)
        wnext = jnp.concatenate([part(1), part(2), part(4)], axis=1)
        zero = jnp.zeros((F_GROUP_DIM, F_GROUP_DIM), F32)
        wm2 = jnp.stack([jnp.block([[w_fmix[i, 2 * g], zero], [zero, w_fmix[i, 2 * g + 1]]])
                         for g in range(F_GROUPS // 2)])
        layers.append(dict(
            ln1=ln1[i].reshape(1, D_MODEL), wmain=wmain.astype(BF16), wnext_self=wnext.astype(BF16),
            sink=sink[i], wm=wm2.astype(BF16), wao=w_ao[i][perm, :].astype(BF16),
            wfo=w_fo[i].astype(BF16), wout=w_out[i].astype(BF16), lnpg=ln_pg[i].reshape(1, D_MODEL),
            wpg=w_pg[i].astype(BF16), wpe=w_pe[i].astype(BF16)))
    for i in range(DEPTH):
        if i + 1 < DEPTH:
            layers[i]["ln_next"] = layers[i + 1]["ln1"]
            layers[i]["wnext"] = layers[i + 1]["wnext_self"]
        else:
            layers[i]["ln_next"] = ln_f.reshape(1, D_MODEL)
    return layers


def _trunk(x, p, layers):
    S = x.shape[1]
    ropeq = jnp.asarray(_rope_tables(S, LOG2E / math.sqrt(HEAD_DIM)))
    ropek = jnp.asarray(_rope_tables(S, 1.0))
    tables = _fft_tables(S)
    pm = _u_perm_matrix()
    kv, u = _prologue(x, layers[0]["ln1"], layers[0]["wnext_self"], ropek, pm)
    h = x
    for i in range(DEPTH):
        w = layers[i]
        fm = _fourier_mix(u, w["wm"], tables)
        last = i + 1 == DEPTH
        outs = _layer(i, last, h, p, fm, kv, ropeq, ropek, pm, w)
        if last:
            h = outs[0]
        else:
            h, kv, u = outs
    return h


def kernel(x_prompt, x_sample, p_prompt, p_sample, ln1, w_in, sink, w_fmix, w_ao, w_fo,
           w_out, w_pe, ln_pg, w_pg, ln_f):
    layers = _prep_weights(ln1, w_in, sink, w_fmix, w_ao, w_fo, w_out, w_pe, ln_pg, w_pg, ln_f)
    return (_trunk(x_prompt, p_prompt, layers), _trunk(x_sample, p_sample, layers))
```

```python
import functools
import math

import numpy as np
import jax
import jax.numpy as jnp
from jax import lax
from jax.experimental import pallas as pl
from jax.experimental.pallas import tpu as pltpu

D_MODEL = 1024
DEPTH = 4
N_HEADS = 8
N_KV_HEADS = 2
HEAD_DIM = 64
ATT_WIDTH = N_HEADS * HEAD_DIM
KV_WIDTH = N_KV_HEADS * HEAD_DIM
WINDOW = 128
ROPE_DIM = 16
ROPE_THETA = 500000.0
F_GROUPS = 4
F_GROUP_DIM = 128
F_WIDTH = F_GROUPS * F_GROUP_DIM
PLE_DIM = 256
EPS = 1e-6
MASK_VALUE = -1e30
LOG2E = math.log2(math.e)

F32 = jnp.float32
BF16 = jnp.bfloat16

LANES = 128
ROW_TILE = 512
FFT_FAST = 32
U_ROWS = ROW_TILE // FFT_FAST
FFT1_LANES = 2048
K1_SUB = 8
K1_STEP = 16
VMEM_LIMIT = 56 * 1024 * 1024

_Q0, _GA0, _GF0, _MGA0, _MGF0, _MAIN_W = 0, 512, 1024, 1536, 2560, 3584
NEXT_W = 2 * KV_WIDTH + F_WIDTH


def _dot(a, b):
    return jnp.dot(a, b, preferred_element_type=F32)


def _sigmoid(x):
    return 0.5 * (jnp.tanh(0.5 * x) + 1.0)


def _silu(x):
    return x * _sigmoid(x)


def _rms(x, g):
    ms = jnp.mean(x * x, axis=-1, keepdims=True)
    return x * lax.rsqrt(ms + EPS) * g


def _rope(x, r_ref):
    c, sa, sb = r_ref[0], r_ref[1], r_ref[2]
    cols = []
    for j in range(x.shape[1] // LANES):
        xj = x[:, j * LANES:(j + 1) * LANES]
        cols.append(xj * c
                    + pltpu.roll(xj, ROPE_DIM // 2, 1) * sa
                    + pltpu.roll(xj, LANES - ROPE_DIM // 2, 1) * sb)
    return cols[0] if len(cols) == 1 else jnp.concatenate(cols, axis=1)


@functools.lru_cache(maxsize=None)
def _rope_tables(S, scale):
    half = ROPE_DIM // 2
    inv = ROPE_THETA ** (-np.arange(0, ROPE_DIM, 2, dtype=np.float64) / ROPE_DIM)
    ang = np.arange(S, dtype=np.float64)[:, None] * inv[None, :]
    cos, sin = np.cos(ang), np.sin(ang)
    c = np.ones((S, LANES))
    sa = np.zeros((S, LANES))
    sb = np.zeros((S, LANES))
    for lane in range(LANES):
        m = lane % HEAD_DIM
        if m < half:
            c[:, lane] = cos[:, m]
            sb[:, lane] = -sin[:, m]
        elif m < ROPE_DIM:
            c[:, lane] = cos[:, m - half]
            sa[:, lane] = sin[:, m - half]
    return (np.stack([c, sa, sb]) * scale).astype(np.float32)


@functools.lru_cache(maxsize=None)
def _fft_tables(S):
    B = FFT_FAST
    A = S // B
    k1 = np.arange(A)
    ang1 = 2.0 * np.pi * ((k1[:, None] * k1[None, :]) % A) / A
    f1 = np.concatenate([np.cos(ang1), -np.sin(ang1)], axis=0) / math.sqrt(A)
    k2 = np.arange(B)
    b = np.arange(B)
    kk = k1[:, None, None] + A * k2[None, :, None]
    ang2 = 2.0 * np.pi * ((kk * b[None, None, :]) % S) / S
    gre = (np.cos(ang2) / math.sqrt(B)).reshape(A // K1_SUB, K1_SUB, B, B)
    gim = (-np.sin(ang2) / math.sqrt(B)).reshape(A // K1_SUB, K1_SUB, B, B)
    lb1 = np.zeros((A // K1_SUB, 2, B, K1_SUB, B, K1_SUB))
    lb2 = np.zeros((A // K1_SUB, 2, B, K1_SUB, B, K1_SUB))
    for r in range(K1_SUB):
        lb1[:, 0, :, r, :, r] = gre[:, r]
        lb1[:, 1, :, r, :, r] = gim[:, r]
        lb2[:, 0, :, r, :, r] = -gim[:, r]
        lb2[:, 1, :, r, :, r] = gre[:, r]
    lb1 = lb1.reshape(A // K1_SUB, 2 * B * K1_SUB, B * K1_SUB)
    lb2 = lb2.reshape(A // K1_SUB, 2 * B * K1_SUB, B * K1_SUB)
    c = np.arange(F_GROUP_DIM)
    ang3 = 2.0 * np.pi * ((c[:, None] * c[None, :]) % F_GROUP_DIM) / F_GROUP_DIM
    zero = np.zeros((F_GROUP_DIM, F_GROUP_DIM))
    cc = np.cos(ang3) / math.sqrt(F_GROUP_DIM)
    sc = np.sin(ang3) / math.sqrt(F_GROUP_DIM)
    cc2 = np.block([[cc, zero], [zero, cc]])
    sc2 = np.block([[sc, zero], [zero, sc]])
    return tuple(np.asarray(t, dtype=BF16) for t in (f1, lb1, lb2, cc2, sc2))


@functools.lru_cache(maxsize=None)
def _u_perm_matrix():
    pm = np.zeros((ROW_TILE, ROW_TILE))
    for a in range(U_ROWS):
        for b in range(FFT_FAST):
            pm[b * U_ROWS + a, a * FFT_FAST + b] = 1.0
    return np.asarray(pm, dtype=BF16)


def _const_spec(shape):
    zeros = (0,) * len(shape)
    return pl.BlockSpec(shape, lambda *_: zeros, pipeline_mode=pl.Buffered(1))


def _fourier_kernel(u_ref, f1_ref, lb1_ref, lb2_ref, cc_ref, sc_ref, wm_ref, o_ref,
                    yre_buf, yim_buf, cw_buf, sw_buf):
    A, width = u_ref.shape
    B = FFT_FAST
    j = pl.program_id(1)

    @pl.when(j == 0)
    def _stage1():
        for g in range(F_GROUPS // 2):
            cw_buf[g] = _dot(cc_ref[...], wm_ref[g]).astype(BF16)
            sw_buf[g] = _dot(sc_ref[...], wm_ref[g]).astype(BF16)
        for lc in range(width // FFT1_LANES):
            lanes = slice(lc * FFT1_LANES, (lc + 1) * FFT1_LANES)
            y = _dot(f1_ref[...], u_ref[:, lanes])
            yre_buf[:, lanes] = y[:A].astype(BF16)
            yim_buf[:, lanes] = y[A:].astype(BF16)

    r0 = pl.multiple_of(j * K1_STEP, K1_STEP)
    yre = yre_buf[pl.ds(r0, K1_STEP), :].astype(F32)
    yim = yim_buf[pl.ds(r0, K1_STEP), :].astype(F32)
    half = B * K1_SUB
    zre, zim = [], []
    for s in range(K1_STEP // K1_SUB):
        rows = slice(s * K1_SUB, (s + 1) * K1_SUB)
        tre = jnp.concatenate([yre[rows, b * F_WIDTH:(b + 1) * F_WIDTH] for b in range(B)], axis=0)
        tim = jnp.concatenate([yim[rows, b * F_WIDTH:(b + 1) * F_WIDTH] for b in range(B)], axis=0)
        z = _dot(lb1_ref[s], tre.astype(BF16)) + _dot(lb2_ref[s], tim.astype(BF16))
        zre.append(z[:half])
        zim.append(z[half:])
    order = lambda parts: jnp.concatenate(
        [p[k2 * K1_SUB:(k2 + 1) * K1_SUB] for k2 in range(B) for p in parts], axis=0).astype(BF16)
    zre = order(zre)
    zim = order(zim)
    pair = 2 * F_GROUP_DIM
    for g in range(F_GROUPS // 2):
        lanes = slice(g * pair, (g + 1) * pair)
        y = (_dot(zre[:, lanes], cw_buf[g]) + _dot(zim[:, lanes], sw_buf[g])).astype(BF16)
        for k2 in range(B):
            o_ref[k2, :, lanes] = y[k2 * K1_STEP:(k2 + 1) * K1_STEP]


def _fourier_mix(u1, wm2, tables):
    f1, lb1, lb2, cc2, sc2 = tables
    nb, A, width = u1.shape
    B = FFT_FAST
    nsub = K1_STEP // K1_SUB
    lspec = pl.BlockSpec((nsub, 2 * B * K1_SUB, B * K1_SUB), lambda i, j: (j, 0, 0))
    pair = 2 * F_GROUP_DIM
    out = pl.pallas_call(
        _fourier_kernel,
        grid=(nb, A // K1_STEP),
        in_specs=[pl.BlockSpec((None, A, width), lambda i, j: (i, 0, 0)),
                  _const_spec((2 * A, A)), lspec, lspec,
                  _const_spec((pair, pair)), _const_spec((pair, pair)),
                  _const_spec((F_GROUPS // 2, pair, pair))],
        out_specs=pl.BlockSpec((None, B, K1_STEP, F_WIDTH), lambda i, j: (i, 0, j, 0)),
        out_shape=jax.ShapeDtypeStruct((nb, B, A, F_WIDTH), BF16),
        scratch_shapes=[pltpu.VMEM((A, width), BF16), pltpu.VMEM((A, width), BF16),
                        pltpu.VMEM((F_GROUPS // 2, pair, pair), BF16),
                        pltpu.VMEM((F_GROUPS // 2, pair, pair), BF16)],
        compiler_params=pltpu.CompilerParams(
            dimension_semantics=("parallel", "arbitrary"), vmem_limit_bytes=VMEM_LIMIT),
        name="fourier",
    )(u1, f1, lb1, lb2, cc2, sc2, wm2)
    return out.reshape(nb, B * A, F_WIDTH)


def _project_next(h, ln_ref, wnext_ref, rk_ref, pm_ref, kv_ref, u_ref):
    hn = _rms(h, ln_ref[...]).astype(BF16)
    z = _dot(hn, wnext_ref[...])
    k = _rope(z[:, :KV_WIDTH], rk_ref)
    kv_ref[:, :KV_WIDTH] = k.astype(BF16)
    kv_ref[:, KV_WIDTH:] = z[:, KV_WIDTH:2 * KV_WIDTH].astype(BF16)
    u = _dot(pm_ref[...], z[:, 2 * KV_WIDTH:].astype(BF16)).astype(BF16)
    for b in range(FFT_FAST):
        u_ref[:, b * F_WIDTH:(b + 1) * F_WIDTH] = u[b * U_ROWS:(b + 1) * U_ROWS]


def _prologue_kernel(x_ref, ln_ref, wnext_ref, rk_ref, pm_ref, kv_ref, u_ref):
    _project_next(x_ref[...], ln_ref, wnext_ref, rk_ref, pm_ref, kv_ref, u_ref)


def _u_spec():
    return pl.BlockSpec((None, U_ROWS, FFT_FAST * F_WIDTH), lambda b, t: (b, t, 0))


def _u_shape(nb, S):
    return jax.ShapeDtypeStruct((nb, S // FFT_FAST, FFT_FAST * F_WIDTH), BF16)


def _prologue(x, ln, wnext, ropek, pm):
    nb, S, _ = x.shape
    tm = ROW_TILE
    row = lambda w: pl.BlockSpec((None, tm, w), lambda b, t: (b, t, 0))
    return pl.pallas_call(
        _prologue_kernel,
        grid=(nb, S // tm),
        in_specs=[row(D_MODEL), _const_spec((1, D_MODEL)), _const_spec((D_MODEL, NEXT_W)),
                  pl.BlockSpec((3, tm, LANES), lambda b, t: (0, t, 0)), _const_spec((tm, tm))],
        out_specs=[row(2 * KV_WIDTH), _u_spec()],
        out_shape=[jax.ShapeDtypeStruct((nb, S, 2 * KV_WIDTH), BF16), _u_shape(nb, S)],
        compiler_params=pltpu.CompilerParams(
            dimension_semantics=("parallel", "parallel"), vmem_limit_bytes=VMEM_LIMIT),
        name="prologue",
    )(x, ln, wnext, ropek, pm)


def _attention_block(j, seq_len, q, kv_buf, sink_ref, attn_buf):
    tm = q.shape[0]
    t = pl.program_id(1)
    rows = slice(j * WINDOW, (j + 1) * WINDOW)
    lane = lax.broadcasted_iota(jnp.int32, (WINDOW, LANES), 1)
    low = lane < HEAD_DIM
    lo_parts, hi_parts = [], []
    for c in range(ATT_WIDTH // LANES):
        blk = q[rows, c * LANES:(c + 1) * LANES]
        lo_parts.append(jnp.where(low, blk, 0.0))
        hi_parts.append(jnp.where(low, 0.0, blk))
    qm = jnp.concatenate(lo_parts + hi_parts, axis=0).astype(BF16)
    kwin = kv_buf[j * WINDOW:j * WINDOW + 3 * WINDOW, :KV_WIDTH]
    vwin = kv_buf[j * WINDOW:j * WINDOW + 3 * WINDOW, KV_WIDTH:]
    s = lax.dot_general(qm, kwin, (((1,), (1,)), ((), ())), preferred_element_type=F32)

    qi = lax.broadcasted_iota(jnp.int32, (WINDOW, WINDOW), 0)
    ki = lax.broadcasted_iota(jnp.int32, (WINDOW, WINDOW), 1)
    first_key = t * tm + (j - 1) * WINDOW
    valid_lo = (ki >= qi) & (first_key >= 0)
    valid_hi = (ki <= qi) & (first_key + 2 * WINDOW < seq_len)

    es, rs = [], []
    for hh in range(N_HEADS):
        sh = s[hh * WINDOW:(hh + 1) * WINDOW]
        s0 = jnp.where(valid_lo, sh[:, :WINDOW], MASK_VALUE)
        s1 = sh[:, WINDOW:2 * WINDOW]
        s2 = jnp.where(valid_hi, sh[:, 2 * WINDOW:], MASK_VALUE)
        sk = sink_ref[hh] * LOG2E
        m = jnp.maximum(jnp.max(jnp.maximum(jnp.maximum(s0, s1), s2), axis=-1, keepdims=True), sk)
        e0, e1, e2 = jnp.exp2(s0 - m), jnp.exp2(s1 - m), jnp.exp2(s2 - m)
        den = jnp.sum(e0 + e1 + e2, axis=-1, keepdims=True) + jnp.exp2(sk - m)
        es.append(jnp.concatenate([e0, e1, e2], axis=1).astype(BF16))
        rs.append(1.0 / den)
    o = _dot(jnp.concatenate(es, axis=0), vwin)
    for c in range(ATT_WIDTH // LANES):
        o_lo = o[c * WINDOW:(c + 1) * WINDOW] * rs[c]
        o_hi = o[(c + 4) * WINDOW:(c + 5) * WINDOW] * rs[c + 4]
        attn_buf[rows, c * LANES:(c + 1) * LANES] = jnp.where(low, o_lo, o_hi)


def _layer_kernel(last, seq_len, *refs):
    if last:
        (h_ref, p_ref, fm_ref, kvp_ref, kvm_ref, kvn_ref, rq_ref, ln1_ref, wmain_ref, sink_ref,
         wao_ref, wfo_ref, wout_ref, lnpg_ref, wpg_ref, wpe_ref, lnn_ref,
         o_ref, kv_buf, attn_buf) = refs
    else:
        (h_ref, p_ref, fm_ref, kvp_ref, kvm_ref, kvn_ref, rq_ref, ln1_ref, wmain_ref, sink_ref,
         wao_ref, wfo_ref, wout_ref, lnpg_ref, wpg_ref, wpe_ref, lnn_ref, wnext_ref, rk_ref, pm_ref,
         o_ref, kvo_ref, uo_ref, kv_buf, attn_buf) = refs
    tm = h_ref.shape[0]

    h = h_ref[...]
    hn = _rms(h, ln1_ref[...]).astype(BF16)

    kv_buf[:WINDOW] = kvp_ref[...]
    kv_buf[WINDOW:WINDOW + tm] = kvm_ref[...]
    kv_buf[WINDOW + tm:] = kvn_ref[...]
    q = _rope(_dot(hn, wmain_ref[:, _Q0:_GA0]), rq_ref)
    f = (fm_ref[...].astype(F32) * _silu(_dot(hn, wmain_ref[:, _GF0:_MGA0]))).astype(BF16)
    pf = _sigmoid(_dot(hn, wmain_ref[:, _MGF0:_MAIN_W])) * _dot(f, wfo_ref[...])
    ga = _silu(_dot(hn, wmain_ref[:, _GA0:_GF0]))
    mga = _sigmoid(_dot(hn, wmain_ref[:, _MGA0:_MGF0]))
    for j in range(tm // WINDOW):
        _attention_block(j, seq_len, q, kv_buf, sink_ref, attn_buf)

    a = (attn_buf[...] * ga).astype(BF16)
    merged = mga * _dot(a, wao_ref[...]) + pf
    h = h + _dot(merged.astype(BF16), wout_ref[...])

    gate = _sigmoid(_dot(_rms(h, lnpg_ref[...]).astype(BF16), wpg_ref[...]))
    h = h + _dot(p_ref[...].astype(BF16), wpe_ref[...]) * gate

    if last:
        o_ref[...] = _rms(h, lnn_ref[...])
    else:
        o_ref[...] = h
        _project_next(h, lnn_ref, wnext_ref, rk_ref, pm_ref, kvo_ref, uo_ref)


def _layer(i, last, h, p, fm, kv, ropeq, ropek, pm, w):
    nb, S, _ = h.shape
    tm = ROW_TILE
    r = tm // WINDOW
    nblk = S // WINDOW
    row = lambda width: pl.BlockSpec((None, tm, width), lambda b, t: (b, t, 0))
    rope_spec = pl.BlockSpec((3, tm, LANES), lambda b, t: (0, t, 0))
    in_specs = [
        row(D_MODEL),
        pl.BlockSpec((None, None, tm, PLE_DIM), lambda b, t: (i, b, t, 0)),
        row(F_WIDTH),
        pl.BlockSpec((None, WINDOW, 2 * KV_WIDTH), lambda b, t: (b, jnp.maximum(t * r - 1, 0), 0)),
        row(2 * KV_WIDTH),
        pl.BlockSpec((None, WINDOW, 2 * KV_WIDTH), lambda b, t: (b, jnp.minimum((t + 1) * r, nblk - 1), 0)),
        rope_spec,
        _const_spec((1, D_MODEL)),
        _const_spec((D_MODEL, _MAIN_W)),
        pl.BlockSpec(memory_space=pltpu.SMEM),
        _const_spec((ATT_WIDTH, D_MODEL)),
        _const_spec((F_WIDTH, D_MODEL)),
        _const_spec((D_MODEL, D_MODEL)),
        _const_spec((1, D_MODEL)),
        _const_spec((D_MODEL, D_MODEL)),
        _const_spec((PLE_DIM, D_MODEL)),
        _const_spec((1, D_MODEL)),
    ]
    args = [h, p, fm, kv, kv, kv, ropeq, w["ln1"], w["wmain"], w["sink"], w["wao"], w["wfo"],
            w["wout"], w["lnpg"], w["wpg"], w["wpe"], w["ln_next"]]
    out_specs = [row(D_MODEL)]
    out_shape = [jax.ShapeDtypeStruct((nb, S, D_MODEL), F32)]
    if not last:
        in_specs += [_const_spec((D_MODEL, NEXT_W)), rope_spec, _const_spec((tm, tm))]
        args += [w["wnext"], ropek, pm]
        out_specs += [row(2 * KV_WIDTH), _u_spec()]
        out_shape += [jax.ShapeDtypeStruct((nb, S, 2 * KV_WIDTH), BF16), _u_shape(nb, S)]
    return pl.pallas_call(
        functools.partial(_layer_kernel, last, S),
        grid=(nb, S // tm),
        in_specs=in_specs,
        out_specs=out_specs,
        out_shape=out_shape,
        scratch_shapes=[pltpu.VMEM((tm + 2 * WINDOW, 2 * KV_WIDTH), BF16),
                        pltpu.VMEM((tm, ATT_WIDTH), F32)],
        compiler_params=pltpu.CompilerParams(
            dimension_semantics=("parallel", "parallel"), vmem_limit_bytes=VMEM_LIMIT),
        name="layer_last" if last else "layer",
    )(*args)


def _head_perm():
    idx = []
    for c in range(ATT_WIDTH // LANES):
        idx += list(range(c * HEAD_DIM, (c + 1) * HEAD_DIM))
        idx += list(range((c + 4) * HEAD_DIM, (c + 5) * HEAD_DIM))
    return np.asarray(idx)


def _prep_weights(ln1, w_in, sink, w_fmix, w_ao, w_fo, w_out, w_pe, ln_pg, w_pg, ln_f):
    perm = _head_perm()
    o = np.cumsum([0, ATT_WIDTH, KV_WIDTH, KV_WIDTH, ATT_WIDTH, F_WIDTH, F_WIDTH, D_MODEL, D_MODEL])
    layers = []
    for i in range(DEPTH):
        wi = w_in[i]
        part = lambda n: wi[:, o[n]:o[n + 1]]
        wmain = jnp.concatenate([part(0)[:, perm], part(3)[:, perm], part(5), part(6), part(7)], axis=1)
        wnext = jnp.concatenate([part(1), part(2), part(4)], axis=1)
        zero = jnp.zeros((F_GROUP_DIM, F_GROUP_DIM), F32)
        wm2 = jnp.stack([jnp.block([[w_fmix[i, 2 * g], zero], [zero, w_fmix[i, 2 * g + 1]]])
                         for g in range(F_GROUPS // 2)])
        layers.append(dict(
            ln1=ln1[i].reshape(1, D_MODEL), wmain=wmain.astype(BF16), wnext_self=wnext.astype(BF16),
            sink=sink[i], wm=wm2.astype(BF16), wao=w_ao[i][perm, :].astype(BF16),
            wfo=w_fo[i].astype(BF16), wout=w_out[i].astype(BF16), lnpg=ln_pg[i].reshape(1, D_MODEL),
            wpg=w_pg[i].astype(BF16), wpe=w_pe[i].astype(BF16)))
    for i in range(DEPTH):
        if i + 1 < DEPTH:
            layers[i]["ln_next"] = layers[i + 1]["ln1"]
            layers[i]["wnext"] = layers[i + 1]["wnext_self"]
        else:
            layers[i]["ln_next"] = ln_f.reshape(1, D_MODEL)
    return layers


def _trunk(x, p, layers):
    S = x.shape[1]
    ropeq = jnp.asarray(_rope_tables(S, LOG2E / math.sqrt(HEAD_DIM)))
    ropek = jnp.asarray(_rope_tables(S, 1.0))
    tables = _fft_tables(S)
    pm = _u_perm_matrix()
    kv, u = _prologue(x, layers[0]["ln1"], layers[0]["wnext_self"], ropek, pm)
    h = x
    for i in range(DEPTH):
        w = layers[i]
        fm = _fourier_mix(u, w["wm"], tables)
        last = i + 1 == DEPTH
        outs = _layer(i, last, h, p, fm, kv, ropeq, ropek, pm, w)
        if last:
            h = outs[0]
        else:
            h, kv, u = outs
    return h


def kernel(x_prompt, x_sample, p_prompt, p_sample, ln1, w_in, sink, w_fmix, w_ao, w_fo,
           w_out, w_pe, ln_pg, w_pg, ln_f):
    layers = _prep_weights(ln1, w_in, sink, w_fmix, w_ao, w_fo, w_out, w_pe, ln_pg, w_pg, ln_f)
    return (_trunk(x_prompt, p_prompt, layers), _trunk(x_sample, p_sample, layers))
```
